```python
import jax, jax.numpy as jnp
from jax import lax
import numpy as np

D_MODEL = 1024
BATCH = 4
SEQ = 4096
DEPTH = 1

CONV_GROUPS = 8
ATTN_HEADS = 8
HEAD_DIM = D_MODEL // (CONV_GROUPS + ATTN_HEADS)
CONV_WIDTH = CONV_GROUPS * HEAD_DIM
ATTN_WIDTH = ATTN_HEADS * HEAD_DIM
MIX_WIDTH = CONV_WIDTH + ATTN_WIDTH
IN_WIDTH = 3 * CONV_WIDTH + 3 * ATTN_WIDTH
CONV_K = 3
MOBA_BLOCK = 256
MOBA_TOPK = 3
ATTN_Q_CHUNK = 64
PEER_HEADS = 8
PEER_NKEYS = 128
PEER_N_EXPERTS = PEER_NKEYS * PEER_NKEYS
PEER_DKEY = 256
PEER_TOPK = 16
PEER_TOKEN_CHUNK = 128
NORM_EPS = 1e-6
NEG_INF = -1e30

kernel_name = "hybrid_conv_moba_peer_block"


def rms_norm(x, gain):
    xf = x.astype(jnp.float32)
    y = xf * lax.rsqrt(jnp.mean(xf * xf, axis=-1, keepdims=True) + NORM_EPS)
    return (y * gain.astype(jnp.float32)).astype(x.dtype)


def group_rms_norm(y, gain):
    shp = y.shape
    yg = y.reshape(shp[:-1] + (shp[-1] // HEAD_DIM, HEAD_DIM)).astype(jnp.float32)
    yg = yg * lax.rsqrt(jnp.mean(yg * yg, axis=-1, keepdims=True) + NORM_EPS)
    return (yg.reshape(shp) * gain.astype(jnp.float32)).astype(y.dtype)


def causal_depthwise_conv(u, w):
    C = u.shape[-1]
    rhs = w[:, None, :].astype(u.dtype)
    return lax.conv_general_dilated(u, rhs, window_strides=(1,), padding=[(CONV_K - 1, 0)],
                                    dimension_numbers=("NWC", "WIO", "NWC"),
                                    feature_group_count=C)


def gather_blocks(blocks, idx):
    return jax.vmap(jax.vmap(lambda t, i: t[i]))(blocks, idx)


def moba_attention(q, k, v):
    B, S, H, dh = q.shape
    nb = -(-S // MOBA_BLOCK)
    pad = nb * MOBA_BLOCK - S
    scale = dh ** -0.5
    qh = q.transpose(0, 2, 1, 3)
    kh = jnp.pad(k.transpose(0, 2, 1, 3), ((0, 0), (0, 0), (0, pad), (0, 0)))
    vh = jnp.pad(v.transpose(0, 2, 1, 3), ((0, 0), (0, 0), (0, pad), (0, 0)))
    kb = kh.reshape(B, H, nb, MOBA_BLOCK, dh)
    vb = vh.reshape(B, H, nb, MOBA_BLOCK, dh)
    qblk = jnp.arange(S) // MOBA_BLOCK
    n_sel = min(MOBA_TOPK, nb - 1)
    nc = S // ATTN_Q_CHUNK
    q_c = qh.reshape(B, H, nc, ATTN_Q_CHUNK, dh).transpose(2, 0, 1, 3, 4)
    xs = (jnp.arange(nc), q_c)
    if n_sel > 0:
        kmean = jnp.mean(kb.astype(jnp.float32), axis=3)
        gate = jnp.einsum("bhsd,bhnd->bhsn", qh.astype(jnp.float32), kmean)
        past = jnp.arange(nb)[None, :] < qblk[:, None]
        gate = jnp.where(past, gate, NEG_INF)
        _, sel = lax.top_k(gate, n_sel)
        valid = sel < qblk[:, None]
        sel_c = sel.reshape(B, H, nc, ATTN_Q_CHUNK, n_sel).transpose(2, 0, 1, 3, 4)
        valid_c = valid.reshape(B, H, nc, ATTN_Q_CHUNK, n_sel).transpose(2, 0, 1, 3, 4)
        xs = xs + (sel_c, valid_c)

    def chunk(args):
        ci, qc = args[0], args[1]
        q0 = ci * ATTN_Q_CHUNK
        blk = q0 // MOBA_BLOCK
        k_own = lax.dynamic_index_in_dim(kb, blk, axis=2, keepdims=False)
        v_own = lax.dynamic_index_in_dim(vb, blk, axis=2, keepdims=False)
        qpos = q0 + jnp.arange(ATTN_Q_CHUNK)
        kpos = blk * MOBA_BLOCK + jnp.arange(MOBA_BLOCK)
        own_mask = kpos[None, :] <= qpos[:, None]
        s_own = jnp.einsum("bhqd,bhkd->bhqk", qc, k_own,
                           preferred_element_type=jnp.float32) * scale
        s_own = jnp.where(own_mask, s_own, NEG_INF)
        if n_sel == 0:
            p_own = jax.nn.softmax(s_own, axis=-1).astype(v_own.dtype)
            return jnp.einsum("bhqk,bhkd->bhqd", p_own, v_own)
        selc, validc = args[2], args[3]
        k_sel = gather_blocks(kb, selc)
        v_sel = gather_blocks(vb, selc)
        s_sel = jnp.einsum("bhqd,bhqnkd->bhqnk", qc, k_sel,
                           preferred_element_type=jnp.float32) * scale
        s_sel = jnp.where(validc[..., None], s_sel, NEG_INF)
        n_past = n_sel * MOBA_BLOCK
        s_all = jnp.concatenate([s_sel.reshape(s_sel.shape[:3] + (n_past,)), s_own], axis=-1)
        p = jax.nn.softmax(s_all, axis=-1).astype(v_own.dtype)
        p_sel = p[..., :n_past].reshape(s_sel.shape)
        p_own = p[..., n_past:]
        return (jnp.einsum("bhqnk,bhqnkd->bhqd", p_sel, v_sel)
                + jnp.einsum("bhqk,bhkd->bhqd", p_own, v_own))

    out = lax.map(chunk, xs)
    return out.transpose(1, 0, 3, 2, 4).reshape(B, S, H * dh)


def hybrid_mixer(h, w_in, conv_w, group_gain, w_out):
    B, S, _ = h.shape
    p = h @ w_in
    cuts = [CONV_WIDTH, 2 * CONV_WIDTH, 3 * CONV_WIDTH,
            3 * CONV_WIDTH + ATTN_WIDTH, 3 * CONV_WIDTH + 2 * ATTN_WIDTH]
    gb, gc, hx, q, k, v = jnp.split(p, cuts, axis=-1)
    y_conv = gb * causal_depthwise_conv(gc * hx, conv_w)
    shp = (B, S, ATTN_HEADS, HEAD_DIM)
    y_attn = moba_attention(q.reshape(shp), k.reshape(shp), v.reshape(shp))
    y = group_rms_norm(jnp.concatenate([y_conv, y_attn], axis=-1), group_gain)
    return y @ w_out


def peer_ffn(h, w_pq, sub_keys, u, v):
    B, S, D = h.shape
    T = B * S
    ht = h.reshape(T, D)
    q = (ht @ w_pq).reshape(T, PEER_HEADS, 2, PEER_DKEY // 2)
    s = jnp.einsum("thpd,hpnd->thpn", q, sub_keys,
                   preferred_element_type=jnp.float32)
    s1, i1 = lax.top_k(s[:, :, 0], PEER_TOPK)
    s2, i2 = lax.top_k(s[:, :, 1], PEER_TOPK)
    cand_s = (s1[..., :, None] + s2[..., None, :]).reshape(T, PEER_HEADS, PEER_TOPK * PEER_TOPK)
    cand_i = (i1[..., :, None] * PEER_NKEYS + i2[..., None, :]).reshape(T, PEER_HEADS, PEER_TOPK * PEER_TOPK)
    top_s, pos = lax.top_k(cand_s, PEER_TOPK)
    eidx = jnp.take_along_axis(cand_i, pos, axis=-1)
    g = jax.nn.softmax(top_s, axis=-1)
    nc = T // PEER_TOKEN_CHUNK

    def chunk(args):
        xc, ec, gcw = args
        a = jnp.einsum("td,thkd->thk", xc, u[ec])
        w = (gcw * jax.nn.gelu(a.astype(jnp.float32))).astype(xc.dtype)
        return jnp.einsum("thk,thkd->td", w, v[ec])

    y = lax.map(chunk, (ht.reshape(nc, PEER_TOKEN_CHUNK, D),
                        eidx.reshape(nc, PEER_TOKEN_CHUNK, PEER_HEADS, PEER_TOPK),
                        g.reshape(nc, PEER_TOKEN_CHUNK, PEER_HEADS, PEER_TOPK)))
    return y.reshape(B, S, D)


def setup_inputs(seed: int = 0) -> dict:
    key = jax.random.key(seed)
    ks = jax.random.split(key, 16)
    D = D_MODEL

    def nrm(k, shape, scale):
        return jax.random.normal(k, shape, jnp.float32) * scale

    return {
        "x": nrm(ks[0], (BATCH, SEQ, D), 1.0),
        "c": nrm(ks[1], (BATCH, D), 1.0),
        "ada_w": nrm(ks[2], (DEPTH, D, 6 * D), 0.5 * D ** -0.5),
        "ada_b": nrm(ks[3], (DEPTH, 6 * D), 0.02),
        "mix_pre_gain": 1.0 + nrm(ks[4], (DEPTH, D), 0.02),
        "mix_post_gain": 1.0 + nrm(ks[5], (DEPTH, D), 0.02),
        "w_in": nrm(ks[6], (DEPTH, D, IN_WIDTH), D ** -0.5),
        "conv_w": nrm(ks[7], (DEPTH, CONV_K, CONV_WIDTH), CONV_K ** -0.5),
        "group_gain": 1.0 + nrm(ks[8], (DEPTH, MIX_WIDTH), 0.02),
        "w_out": nrm(ks[9], (DEPTH, MIX_WIDTH, D), MIX_WIDTH ** -0.5),
        "ffn_pre_gain": 1.0 + nrm(ks[10], (DEPTH, D), 0.02),
        "ffn_post_gain": 1.0 + nrm(ks[11], (DEPTH, D), 0.02),
        "peer_wq": nrm(ks[12], (DEPTH, D, PEER_HEADS * PEER_DKEY), D ** -0.5),
        "peer_keys": nrm(ks[13], (DEPTH, PEER_HEADS, 2, PEER_NKEYS, PEER_DKEY // 2), (PEER_DKEY // 2) ** -0.5),
        "peer_u": nrm(ks[14], (DEPTH, PEER_N_EXPERTS, D), D ** -0.5),
        "peer_v": nrm(ks[15], (DEPTH, PEER_N_EXPERTS, D), D ** -0.5),
    }


def reference(x, c, ada_w, ada_b, mix_pre_gain, mix_post_gain, w_in, conv_w, group_gain, w_out,
              ffn_pre_gain, ffn_post_gain, peer_wq, peer_keys, peer_u, peer_v):
    for l in range(DEPTH):
        mod = (jax.nn.silu(c) @ ada_w[l] + ada_b[l])[:, None, :]
        sh1, sc1, g1, sh2, sc2, g2 = jnp.split(mod, 6, axis=-1)
        h = rms_norm(x, mix_pre_gain[l]) * (1.0 + sc1) + sh1
        y = hybrid_mixer(h, w_in[l], conv_w[l], group_gain[l], w_out[l])
        x = x + g1 * rms_norm(y, mix_post_gain[l])
        h = rms_norm(x, ffn_pre_gain[l]) * (1.0 + sc2) + sh2
        y = peer_ffn(h, peer_wq[l], peer_keys[l], peer_u[l], peer_v[l])
        x = x + g2 * rms_norm(y, ffn_post_gain[l])
    return x
```

```python
import functools

import jax
import jax.numpy as jnp
from jax import lax
from jax.experimental import pallas as pl
from jax.experimental.pallas import tpu as pltpu

F32 = jnp.float32
BF16 = jnp.bfloat16

HEAD_DIM = 64
MOBA_BLOCK = 256
MOBA_TOPK = 3
PEER_TOPK = 16
NORM_EPS = 1e-6
NEG_INF = -1e30
LANES = 128
VMEM_LIMIT = 56 * 1024 * 1024


def _params(*sem):
    return pltpu.CompilerParams(dimension_semantics=sem, vmem_limit_bytes=VMEM_LIMIT)


def _mod_kernel(c_ref, w_ref, b_ref, o_ref):
    c = c_ref[...]
    s = c * jax.nn.sigmoid(c)
    o_ref[...] = jnp.dot(s, w_ref[...], preferred_element_type=F32,
                         precision=lax.Precision.HIGHEST) + b_ref[...]


def _modulation(c_pad, w, b):
    rows, d = c_pad.shape
    n = w.shape[1]
    tn = 768 if n % 768 == 0 else n
    return pl.pallas_call(
        _mod_kernel,
        grid=(n // tn,),
        in_specs=[pl.BlockSpec((rows, d), lambda j: (0, 0)),
                  pl.BlockSpec((d, tn), lambda j: (0, j)),
                  pl.BlockSpec((1, tn), lambda j: (0, j))],
        out_specs=pl.BlockSpec((rows, tn), lambda j: (0, j)),
        out_shape=jax.ShapeDtypeStruct((rows, n), F32),
        compiler_params=_params("parallel"),
        name="adaln_mod",
    )(c_pad, w, b.reshape(1, n))


def _inproj_kernel(x_ref, sh_ref, sc_ref, gain_ref, wt_ref, wk_ref, cw_ref,
                   yc_ref, q_ref, k_ref, v_ref, carry_ref, *, tps, cw, aw):
    i = pl.program_id(0)
    x = x_ref[...]
    tm = x.shape[0]
    ms = jnp.mean(x * x, axis=-1, keepdims=True)
    h = x * lax.rsqrt(ms + NORM_EPS) * gain_ref[...]
    h = h * (1.0 + sc_ref[...]) + sh_ref[...]
    hb = h.astype(BF16)
    k_ref[...] = jnp.dot(hb, wk_ref[...], preferred_element_type=F32).astype(BF16)
    pT = lax.dot_general(wt_ref[...], hb, (((1,), (1,)), ((), ())),
                         preferred_element_type=F32)
    gate_b = pT[0:cw]
    gate_c = pT[cw:2 * cw]
    hx = pT[2 * cw:3 * cw]
    q_ref[...] = pT[3 * cw:3 * cw + aw].astype(BF16)
    vT = pT[3 * cw + aw:3 * cw + 2 * aw]
    for j in range(tm // MOBA_BLOCK):
        v_ref[j] = vT[:, j * MOBA_BLOCK:(j + 1) * MOBA_BLOCK].astype(BF16)
    u = gate_c * hx
    prev = jnp.where(i % tps == 0, 0.0, carry_ref[...])
    carry_ref[...] = u[:, tm - LANES:]
    ue = jnp.concatenate([prev, u], axis=1)
    u1 = pltpu.roll(ue, 1, axis=1)[:, LANES:]
    u2 = pltpu.roll(ue, 2, axis=1)[:, LANES:]
    conv = cw_ref[0] * u2 + cw_ref[1] * u1 + cw_ref[2] * u
    yc_ref[...] = gate_b * conv


def _inproj(x2, mod_rows, gain_row, w_t, w_k, conv_cols, *, seq, tm):
    t, d = x2.shape
    cw = conv_cols.shape[1]
    aw = w_k.shape[1]
    tps = seq // tm
    nrest = w_t.shape[0]
    kern = functools.partial(_inproj_kernel, tps=tps, cw=cw, aw=aw)
    return pl.pallas_call(
        kern,
        grid=(t // tm,),
        in_specs=[
            pl.BlockSpec((tm, d), lambda i: (i, 0)),
            pl.BlockSpec((None, None, 1, d), lambda i: (i // tps, 0, 0, 0)),
            pl.BlockSpec((None, None, 1, d), lambda i: (i // tps, 1, 0, 0)),
            pl.BlockSpec((1, d), lambda i: (0, 0)),
            pl.BlockSpec((nrest, d), lambda i: (0, 0)),
            pl.BlockSpec((d, aw), lambda i: (0, 0)),
            pl.BlockSpec((3, cw, 1), lambda i: (0, 0, 0)),
        ],
        out_specs=[
            pl.BlockSpec((cw, tm), lambda i: (0, i)),
            pl.BlockSpec((aw, tm), lambda i: (0, i)),
            pl.BlockSpec((tm, aw), lambda i: (i, 0)),
            pl.BlockSpec((tm // MOBA_BLOCK, aw, MOBA_BLOCK), lambda i: (i, 0, 0)),
        ],
        out_shape=[
            jax.ShapeDtypeStruct((cw, t), F32),
            jax.ShapeDtypeStruct((aw, t), BF16),
            jax.ShapeDtypeStruct((t, aw), BF16),
            jax.ShapeDtypeStruct((t // MOBA_BLOCK, aw, MOBA_BLOCK), BF16),
        ],
        scratch_shapes=[pltpu.VMEM((cw, LANES), F32)],
        compiler_params=_params("arbitrary"),
        name="inproj_conv",
    )(x2, mod_rows, mod_rows, gain_row, w_t, w_k, conv_cols)


def _moba_kernel(q_ref, k_ref, v_ref, o_ref, km_ref, sel_ref, *, nb, n_sel):
    qb = pl.program_id(2)
    blk = MOBA_BLOCK

    @pl.when(qb == 0)
    def _():
        for n in range(nb):
            km_ref[pl.ds(n, 1), :] = jnp.mean(k_ref[n].astype(F32), axis=0, keepdims=True)

    q2 = q_ref[...].astype(F32)
    row = lax.broadcasted_iota(jnp.int32, q2.shape, 0)
    km = km_ref[...]
    km_hi = km.astype(BF16)
    km_lo = (km - km_hi.astype(F32)).astype(BF16)
    nidx = lax.broadcasted_iota(jnp.int32, (nb, blk), 0)
    krow = lax.broadcasted_iota(jnp.int32, (blk, blk), 0)
    qcol = lax.broadcasted_iota(jnp.int32, (blk, blk), 1)
    causal = krow <= qcol
    scale = HEAD_DIM ** -0.5

    for hh in range(2):
        qm = jnp.where((row >= hh * HEAD_DIM) & (row < (hh + 1) * HEAD_DIM), q2, 0.0).astype(BF16)
        gate = (jnp.dot(km_hi, qm, preferred_element_type=F32)
                + jnp.dot(km_lo, qm, preferred_element_type=F32))
        past = nidx < qb
        g = jnp.where(past, gate, NEG_INF)
        sel = jnp.zeros((nb, blk), F32)
        for _ in range(n_sel):
            mx = jnp.max(g, axis=0, keepdims=True)
            idx = jnp.min(jnp.where(g == mx, nidx, nb), axis=0, keepdims=True)
            pick = nidx == idx
            sel = jnp.where(pick, 1.0, sel)
            g = jnp.where(pick, -3e38, g)
        sel_ref[hh] = jnp.where(past, sel, 0.0)

        s = jnp.dot(k_ref[qb], qm, preferred_element_type=F32) * scale
        s = jnp.where(causal, s, NEG_INF)
        m0 = jnp.max(s, axis=0, keepdims=True)
        p = jnp.exp(s - m0)
        l0 = jnp.sum(p, axis=0, keepdims=True)
        acc0 = jnp.dot(v_ref[qb, hh * HEAD_DIM:(hh + 1) * HEAD_DIM, :], p.astype(BF16),
                       preferred_element_type=F32)

        def body(n, carry, hh=hh, qm=qm):
            m, l, acc = carry
            s = jnp.dot(k_ref[n], qm, preferred_element_type=F32) * scale
            s = jnp.where(sel_ref[hh, pl.ds(n, 1), :] > 0.5, s, NEG_INF)
            m_new = jnp.maximum(m, jnp.max(s, axis=0, keepdims=True))
            alpha = jnp.exp(m - m_new)
            p = jnp.exp(s - m_new)
            l = alpha * l + jnp.sum(p, axis=0, keepdims=True)
            acc = alpha * acc + jnp.dot(v_ref[n, hh * HEAD_DIM:(hh + 1) * HEAD_DIM, :],
                                        p.astype(BF16), preferred_element_type=F32)
            return m_new, l, acc

        _, l, acc = lax.fori_loop(0, qb, body, (m0, l0, acc0))
        o_ref[hh * HEAD_DIM:(hh + 1) * HEAD_DIM, :] = acc / l


def _moba(qT, k_blk, v_blk, *, batch, seq):
    aw, t = qT.shape
    nb = seq // MOBA_BLOCK
    n_sel = min(MOBA_TOPK, nb - 1)
    npair = aw // (2 * HEAD_DIM)
    kern = functools.partial(_moba_kernel, nb=nb, n_sel=n_sel)
    return pl.pallas_call(
        kern,
        grid=(batch, npair, nb),
        in_specs=[
            pl.BlockSpec((2 * HEAD_DIM, MOBA_BLOCK), lambda b, hp, qb: (hp, b * nb + qb)),
            pl.BlockSpec((nb, MOBA_BLOCK, 2 * HEAD_DIM), lambda b, hp, qb: (b, 0, hp)),
            pl.BlockSpec((nb, 2 * HEAD_DIM, MOBA_BLOCK), lambda b, hp, qb: (b, hp, 0)),
        ],
        out_specs=pl.BlockSpec((2 * HEAD_DIM, MOBA_BLOCK), lambda b, hp, qb: (hp, b * nb + qb)),
        out_shape=jax.ShapeDtypeStruct((aw, t), F32),
        scratch_shapes=[pltpu.VMEM((nb, 2 * HEAD_DIM), F32),
                        pltpu.VMEM((2, nb, MOBA_BLOCK), F32)],
        compiler_params=_params("parallel", "parallel", "arbitrary"),
        name="moba_attn",
    )(qT, k_blk, v_blk)


def _rms_rows(v):
    return v * lax.rsqrt(jnp.mean(v * v, axis=0, keepdims=True) + NORM_EPS)


def _outproj_kernel(yc_ref, ya_ref, x_ref, gg_ref, wo_ref, pg_ref, g1_ref, fg_ref, sc2_ref,
                    sh2_ref, wq_ref, keys_ref, x1_ref, h2_ref, s_ref):
    y = jnp.concatenate([yc_ref[...], ya_ref[...]], axis=0)
    ngroups = y.shape[0] // HEAD_DIM
    yn = jnp.concatenate(
        [_rms_rows(y[g * HEAD_DIM:(g + 1) * HEAD_DIM]) for g in range(ngroups)], axis=0)
    yn = (yn * gg_ref[...]).astype(BF16)
    o = jnp.dot(wo_ref[...], yn, preferred_element_type=F32)
    r = _rms_rows(o) * pg_ref[...]
    x1 = x_ref[...].T + g1_ref[...] * r
    x1_ref[...] = x1
    h2 = (_rms_rows(x1) * fg_ref[...]) * (1.0 + sc2_ref[...]) + sh2_ref[...]
    h2b = h2.astype(BF16)
    h2_ref[...] = h2b
    qp = jnp.dot(wq_ref[...], h2b, preferred_element_type=F32)
    nk = keys_ref.shape[1]
    dk = keys_ref.shape[2]
    for hp in range(keys_ref.shape[0]):
        s_ref[hp] = jnp.dot(keys_ref[hp], qp[hp * dk:(hp + 1) * dk].astype(BF16),
                            preferred_element_type=F32)
    del nk


def _outproj(ycT, yaT, x2, gg_col, w_outT, pg_col, mod_cols, fg_col, w_pqT, keys, *, seq, tm):
    t, d = x2.shape
    cw = ycT.shape[0]
    aw = yaT.shape[0]
    tps = seq // tm
    nhp, nk, dk = keys.shape
    col = pl.BlockSpec((d, 1), lambda i: (0, 0))

    def modcol(kidx):
        return pl.BlockSpec((None, None, d, 1), lambda i: (i // tps, kidx, 0, 0))

    return pl.pallas_call(
        _outproj_kernel,
        grid=(t // tm,),
        in_specs=[
            pl.BlockSpec((cw, tm), lambda i: (0, i)),
            pl.BlockSpec((aw, tm), lambda i: (0, i)),
            pl.BlockSpec((tm, d), lambda i: (i, 0)),
            col,
            pl.BlockSpec((d, d), lambda i: (0, 0)),
            col,
            modcol(2),
            col,
            modcol(4),
            modcol(3),
            pl.BlockSpec((nhp * dk, d), lambda i: (0, 0)),
            pl.BlockSpec((nhp, nk, dk), lambda i: (0, 0, 0)),
        ],
        out_specs=[
            pl.BlockSpec((d, tm), lambda i: (0, i)),
            pl.BlockSpec((d, tm), lambda i: (0, i)),
            pl.BlockSpec((nhp, nk, tm), lambda i: (0, 0, i)),
        ],
        out_shape=[
            jax.ShapeDtypeStruct((d, t), F32),
            jax.ShapeDtypeStruct((d, t), BF16),
            jax.ShapeDtypeStruct((nhp, nk, t), F32),
        ],
        compiler_params=_params("parallel"),
        name="outproj_peerq",
    )(ycT, yaT, x2, gg_col, w_outT, pg_col, mod_cols, fg_col, mod_cols, mod_cols, w_pqT, keys)


def _top_values(s, k):
    rows = []
    for _ in range(k):
        mx = jnp.max(s, axis=0, keepdims=True)
        rows.append(mx)
        s = jnp.where(s == mx, -jnp.inf, s)
    return jnp.concatenate(rows, axis=0)


def _route_kernel(s_ref, thr_ref, e1_ref, e2_ref):
    k = PEER_TOPK
    k1 = k + 1
    s1 = s_ref[0]
    s2 = s_ref[1]
    t1 = _top_values(s1, k1)
    t2 = _top_values(s2, k1)
    m1 = t1[0:1]
    m2 = t2[0:1]
    brow8 = lax.broadcasted_iota(jnp.int32, (8, s1.shape[1]), 0)
    cands = [t1[0:1] + t2]
    for a in range(1, k1):
        cands.append(jnp.where(brow8 < k1 // (a + 1), t1[a:a + 1] + t2[0:8], -jnp.inf))
    c = jnp.concatenate(cands, axis=0)
    cnt = jnp.zeros_like(m1)
    ck = jnp.full_like(m1, -jnp.inf)
    ck1 = jnp.full_like(m1, -jnp.inf)
    for _ in range(k1):
        mx = jnp.max(c, axis=0, keepdims=True)
        eq = c == mx
        new_cnt = cnt + jnp.sum(jnp.where(eq, 1.0, 0.0), axis=0, keepdims=True)
        ck = jnp.where(cnt < k, jnp.where(new_cnt >= k, mx, ck), ck)
        ck1 = jnp.where(cnt < k1, jnp.where(new_cnt >= k1, mx, ck1), ck1)
        cnt = new_cnt
        c = jnp.where(eq, -jnp.inf, c)
    tau = 0.5 * (ck + ck1)
    et1 = jnp.exp(t1 - m1)
    et2 = jnp.exp(t2 - m2)
    z = jnp.sum(jnp.where(t2 >= tau - t1[0:1], et2, 0.0), axis=0, keepdims=True) * et1[0:1]
    for a in range(1, k1):
        wa = jnp.where(brow8 < k1 // (a + 1),
                       jnp.where(t2[0:8] >= tau - t1[a:a + 1], et2[0:8], 0.0), 0.0)
        z = z + jnp.sum(wa, axis=0, keepdims=True) * et1[a:a + 1]
    thr_ref[...] = tau - s1
    e1_ref[...] = jnp.exp(s1 - m1) / z
    e2_ref[...] = jnp.exp(s2 - m2)


def _route(s4, *, tl):
    nh, _, nk, t = s4.shape
    out = jax.ShapeDtypeStruct((nh, nk, t), F32)
    ospec = pl.BlockSpec((None, nk, tl), lambda i, h: (h, 0, i))
    return pl.pallas_call(
        _route_kernel,
        grid=(t // tl, nh),
        in_specs=[pl.BlockSpec((None, 2, nk, tl), lambda i, h: (h, 0, 0, i))],
        out_specs=[ospec, ospec, ospec],
        out_shape=[out, out, out],
        compiler_params=_params("parallel", "parallel"),
        name="peer_route",
    )(s4)


def _gelu_tanh(a):
    return 0.5 * a * (1.0 + jnp.tanh(0.7978845608028654 * (a + 0.044715 * (a * a * a))))


def _peer_kernel(h_ref, u_ref, vt_ref, s2_ref, thr_ref, e1_ref, e2_ref, y_ref, w_ref, *, lane_chunk):
    e = pl.program_id(1)
    et, tt = w_ref.shape
    nk = s2_ref.shape[1]
    nheads = s2_ref.shape[0]
    ni = et // nk

    @pl.when(e == 0)
    def _():
        y_ref[...] = jnp.zeros_like(y_ref)

    aT = jnp.dot(u_ref[...], h_ref[...], preferred_element_type=F32)
    for il in range(ni):
        i = e * ni + il
        for c0 in range(0, tt, lane_chunk):
            cs = slice(c0, c0 + lane_chunk)
            g = jnp.zeros((nk, lane_chunk), F32)
            for h in range(nheads):
                thr = thr_ref[h, pl.ds(i, 1), cs]
                e1 = e1_ref[h, pl.ds(i, 1), cs]
                g = g + jnp.where(s2_ref[h, :, cs] >= thr, e2_ref[h, :, cs], 0.0) * e1
            a = aT[il * nk:(il + 1) * nk, cs]
            w_ref[il * nk:(il + 1) * nk, cs] = (g * _gelu_tanh(a)).astype(BF16)
    y_ref[...] += jnp.dot(vt_ref[...], w_ref[...], preferred_element_type=F32)


def _peer(h2T, u_b, vt_b, s4, thr, e1, e2, *, tt, et):
    d, t = h2T.shape
    ne = u_b.shape[0]
    nh, _, nk, _ = s4.shape
    stat = pl.BlockSpec((nh, nk, tt), lambda i, e: (0, 0, i))
    kern = functools.partial(_peer_kernel, lane_chunk=min(256, tt))
    return pl.pallas_call(
        kern,
        grid=(t // tt, ne // et),
        in_specs=[
            pl.BlockSpec((d, tt), lambda i, e: (0, i)),
            pl.BlockSpec((et, d), lambda i, e: (e, 0)),
            pl.BlockSpec((d, et), lambda i, e: (0, e)),
            pl.BlockSpec((nh, None, nk, tt), lambda i, e: (0, 1, 0, i)),
            stat, stat, stat,
        ],
        out_specs=pl.BlockSpec((d, tt), lambda i, e: (0, i)),
        out_shape=jax.ShapeDtypeStruct((d, t), F32),
        scratch_shapes=[pltpu.VMEM((et, tt), BF16)],
        compiler_params=_params("parallel", "arbitrary"),
        name="peer_dense",
    )(h2T, u_b, vt_b, s4, thr, e1, e2)


def _final_kernel(x1_ref, y_ref, pg_ref, g2_ref, o_ref):
    out = x1_ref[...] + g2_ref[...] * (_rms_rows(y_ref[...]) * pg_ref[...])
    o_ref[...] = out.T


def _final(x1T, yT, pg_col, mod_cols, *, seq, tm):
    d, t = x1T.shape
    tps = seq // tm
    return pl.pallas_call(
        _final_kernel,
        grid=(t // tm,),
        in_specs=[
            pl.BlockSpec((d, tm), lambda i: (0, i)),
            pl.BlockSpec((d, tm), lambda i: (0, i)),
            pl.BlockSpec((d, 1), lambda i: (0, 0)),
            pl.BlockSpec((None, None, d, 1), lambda i: (i // tps, 5, 0, 0)),
        ],
        out_specs=pl.BlockSpec((tm, d), lambda i: (i, 0)),
        out_shape=jax.ShapeDtypeStruct((t, d), F32),
        compiler_params=_params("parallel"),
        name="final_residual",
    )(x1T, yT, pg_col, mod_cols)


def _layer(x2, c_pad, ada_w, ada_b, mix_pre_gain, mix_post_gain, w_in, conv_w, group_gain, w_out,
           ffn_pre_gain, ffn_post_gain, peer_wq, peer_keys, peer_u, peer_v, *, batch, seq):
    t, d = x2.shape
    cw = conv_w.shape[-1]
    aw = d - cw
    tm = min(512, seq)

    mod = _modulation(c_pad, ada_w, ada_b)[:batch]
    mod_rows = mod.reshape(batch, 6, 1, d)
    mod_cols = mod.reshape(batch, 6, d, 1)

    w_rest = jnp.concatenate([w_in[:, :3 * cw + aw], w_in[:, 3 * cw + 2 * aw:]], axis=1)
    w_t = w_rest.T.astype(BF16)
    w_k = w_in[:, 3 * cw + aw:3 * cw + 2 * aw].astype(BF16)
    conv_cols = conv_w.reshape(conv_w.shape[0], cw, 1)
    w_outT = w_out.T.astype(BF16)
    w_pqT = peer_wq.T.astype(BF16)
    nh, _, nk, dk = peer_keys.shape
    keys = peer_keys.reshape(nh * 2, nk, dk).astype(BF16)
    u_b = peer_u.astype(BF16)
    vt_b = peer_v.T.astype(BF16)

    ycT, qT, k_nat, v_blk = _inproj(x2, mod_rows, mix_pre_gain.reshape(1, d), w_t, w_k, conv_cols,
                                    seq=seq, tm=tm)
    k_blk = k_nat.reshape(t // MOBA_BLOCK, MOBA_BLOCK, aw)
    yaT = _moba(qT, k_blk, v_blk, batch=batch, seq=seq)
    x1T, h2T, sT = _outproj(ycT, yaT, x2, group_gain.reshape(d, 1), w_outT,
                            mix_post_gain.reshape(d, 1), mod_cols, ffn_pre_gain.reshape(d, 1),
                            w_pqT, keys, seq=seq, tm=tm)
    s4 = sT.reshape(nh, 2, nk, t)
    thr, e1, e2 = _route(s4, tl=min(512, t))
    yT = _peer(h2T, u_b, vt_b, s4, thr, e1, e2, tt=min(512, t), et=min(512, u_b.shape[0]))
    return _final(x1T, yT, ffn_post_gain.reshape(d, 1), mod_cols, seq=seq, tm=tm)


def kernel(x, c, ada_w, ada_b, mix_pre_gain, mix_post_gain, w_in, conv_w, group_gain, w_out,
           ffn_pre_gain, ffn_post_gain, peer_wq, peer_keys, peer_u, peer_v):
    batch, seq, d = x.shape
    x2 = x.reshape(batch * seq, d)
    c_pad = jnp.zeros((8, d), c.dtype).at[:batch].set(c)
    for l in range(ada_w.shape[0]):
        x2 = _layer(x2, c_pad, ada_w[l], ada_b[l], mix_pre_gain[l], mix_post_gain[l], w_in[l],
                    conv_w[l], group_gain[l], w_out[l], ffn_pre_gain[l], ffn_post_gain[l],
                    peer_wq[l], peer_keys[l], peer_u[l], peer_v[l], batch=batch, seq=seq)
    return x2.reshape(batch, seq, d)
```

```python
import functools
import math

import jax
import jax.numpy as jnp
from jax import lax
from jax.experimental import pallas as pl
from jax.experimental.pallas import tpu as pltpu

F32 = jnp.float32
BF16 = jnp.bfloat16

HEAD_DIM = 64
MOBA_BLOCK = 256
MOBA_TOPK = 3
PEER_TOPK = 16
NORM_EPS = 1e-6
NEG_INF = -1e30
LANES = 128
VMEM_LIMIT = 56 * 1024 * 1024


def _params(*sem):
    return pltpu.CompilerParams(dimension_semantics=sem, vmem_limit_bytes=VMEM_LIMIT)


def _mod_kernel(c_ref, w_ref, b_ref, o_ref):
    c = c_ref[...]
    s = c * jax.nn.sigmoid(c)
    o_ref[...] = jnp.dot(s, w_ref[...], preferred_element_type=F32,
                         precision=lax.Precision.HIGHEST) + b_ref[...]


def _modulation(c_pad, w, b):
    rows, d = c_pad.shape
    n = w.shape[1]
    tn = 768 if n % 768 == 0 else n
    return pl.pallas_call(
        _mod_kernel,
        grid=(n // tn,),
        in_specs=[pl.BlockSpec((rows, d), lambda j: (0, 0)),
                  pl.BlockSpec((d, tn), lambda j: (0, j)),
                  pl.BlockSpec((1, tn), lambda j: (0, j))],
        out_specs=pl.BlockSpec((rows, tn), lambda j: (0, j)),
        out_shape=jax.ShapeDtypeStruct((rows, n), F32),
        compiler_params=_params("parallel"),
        name="adaln_mod",
    )(c_pad, w, b.reshape(1, n))


def _inproj_kernel(x_ref, sh_ref, sc_ref, gain_ref, wt_ref, wk_ref, cw_ref,
                   yc_ref, q_ref, k_ref, v_ref, carry_ref, *, tps, cw, aw):
    i = pl.program_id(0)
    x = x_ref[...]
    tm = x.shape[0]
    ms = jnp.mean(x * x, axis=-1, keepdims=True)
    h = x * lax.rsqrt(ms + NORM_EPS) * gain_ref[...]
    h = h * (1.0 + sc_ref[...]) + sh_ref[...]
    hb = h.astype(BF16)
    k_ref[...] = jnp.dot(hb, wk_ref[...], preferred_element_type=F32).astype(BF16)
    pT = lax.dot_general(wt_ref[...], hb, (((1,), (1,)), ((), ())),
                         preferred_element_type=F32)
    gate_b = pT[0:cw]
    gate_c = pT[cw:2 * cw]
    hx = pT[2 * cw:3 * cw]
    q_ref[...] = pT[3 * cw:3 * cw + aw].astype(BF16)
    vT = pT[3 * cw + aw:3 * cw + 2 * aw]
    for j in range(tm // MOBA_BLOCK):
        v_ref[j] = vT[:, j * MOBA_BLOCK:(j + 1) * MOBA_BLOCK].astype(BF16)
    u = gate_c * hx
    prev = jnp.where(i % tps == 0, 0.0, carry_ref[...])
    carry_ref[...] = u[:, tm - LANES:]
    ue = jnp.concatenate([prev, u], axis=1)
    u1 = pltpu.roll(ue, 1, axis=1)[:, LANES:]
    u2 = pltpu.roll(ue, 2, axis=1)[:, LANES:]
    conv = cw_ref[0] * u2 + cw_ref[1] * u1 + cw_ref[2] * u
    yc_ref[...] = gate_b * conv


def _inproj(x2, mod_rows, gain_row, w_t, w_k, conv_cols, *, seq, tm):
    t, d = x2.shape
    cw = conv_cols.shape[1]
    aw = w_k.shape[1]
    tps = seq // tm
    nrest = w_t.shape[0]
    kern = functools.partial(_inproj_kernel, tps=tps, cw=cw, aw=aw)
    return pl.pallas_call(
        kern,
        grid=(t // tm,),
        in_specs=[
            pl.BlockSpec((tm, d), lambda i: (i, 0)),
            pl.BlockSpec((None, None, 1, d), lambda i: (i // tps, 0, 0, 0)),
            pl.BlockSpec((None, None, 1, d), lambda i: (i // tps, 1, 0, 0)),
            pl.BlockSpec((1, d), lambda i: (0, 0)),
            pl.BlockSpec((nrest, d), lambda i: (0, 0)),
            pl.BlockSpec((d, aw), lambda i: (0, 0)),
            pl.BlockSpec((3, cw, 1), lambda i: (0, 0, 0)),
        ],
        out_specs=[
            pl.BlockSpec((cw, tm), lambda i: (0, i)),
            pl.BlockSpec((aw, tm), lambda i: (0, i)),
            pl.BlockSpec((tm, aw), lambda i: (i, 0)),
            pl.BlockSpec((tm // MOBA_BLOCK, aw, MOBA_BLOCK), lambda i: (i, 0, 0)),
        ],
        out_shape=[
            jax.ShapeDtypeStruct((cw, t), F32),
            jax.ShapeDtypeStruct((aw, t), BF16),
            jax.ShapeDtypeStruct((t, aw), BF16),
            jax.ShapeDtypeStruct((t // MOBA_BLOCK, aw, MOBA_BLOCK), BF16),
        ],
        scratch_shapes=[pltpu.VMEM((cw, LANES), F32)],
        compiler_params=_params("arbitrary"),
        name="inproj_conv",
    )(x2, mod_rows, mod_rows, gain_row, w_t, w_k, conv_cols)


def _moba_kernel(q_ref, k_ref, v_ref, o_ref, km_ref, sel_ref, qm_ref, *, nb, n_sel):
    qb = pl.program_id(2)
    blk = MOBA_BLOCK
    scale = HEAD_DIM ** -0.5
    fold_scale = float(math.log2(scale)).is_integer()
    post_scale = 1.0 if fold_scale else scale

    @pl.when(qb == 0)
    def _():
        for n in range(nb):
            km_ref[pl.ds(n, 1), :] = jnp.mean(k_ref[n].astype(F32), axis=0, keepdims=True)

    q2 = q_ref[...].astype(F32)
    if fold_scale:
        q2 = q2 * scale
    row = lax.broadcasted_iota(jnp.int32, q2.shape, 0)
    km = km_ref[...]
    km_hi = km.astype(BF16)
    km_lo = (km - km_hi.astype(F32)).astype(BF16)
    nidx = lax.broadcasted_iota(jnp.int32, (nb, blk), 0)
    past = nidx < qb
    causal = (lax.broadcasted_iota(jnp.int32, (blk, blk), 0)
              <= lax.broadcasted_iota(jnp.int32, (blk, blk), 1))

    def scores(n, hh):
        s = jnp.dot(k_ref[n], qm_ref[hh], preferred_element_type=F32)
        return s if fold_scale else s * post_scale

    def pv(n, hh, p):
        return jnp.dot(v_ref[n, hh * HEAD_DIM:(hh + 1) * HEAD_DIM, :], p.astype(BF16),
                       preferred_element_type=F32)

    init = []
    for hh in range(2):
        qm = jnp.where((row >= hh * HEAD_DIM) & (row < (hh + 1) * HEAD_DIM), q2, 0.0).astype(BF16)
        qm_ref[hh] = qm
        gate = (jnp.dot(km_hi, qm, preferred_element_type=F32)
                + jnp.dot(km_lo, qm, preferred_element_type=F32))
        g = jnp.where(past, gate, NEG_INF)
        sel = jnp.zeros((nb, blk), F32)
        for _ in range(n_sel):
            mx = jnp.max(g, axis=0, keepdims=True)
            idx = jnp.min(jnp.where(g == mx, nidx, nb), axis=0, keepdims=True)
            pick = nidx == idx
            sel = jnp.where(pick, 1.0, sel)
            g = jnp.where(pick, -3e38, g)
        sel_ref[hh] = jnp.where(past, sel, 0.0)
        s = jnp.where(causal, scores(qb, hh), NEG_INF)
        m0 = jnp.max(s, axis=0, keepdims=True)
        p = jnp.exp(s - m0)
        init.append((m0, jnp.sum(p, axis=0, keepdims=True), pv(qb, hh, p)))

    def body(j, carry):
        out = []
        for hh in range(2):
            m, l, acc = carry[hh]
            n0 = 2 * j
            n1 = n0 + 1
            s0 = jnp.where(sel_ref[hh, pl.ds(n0, 1), :] > 0.5, scores(n0, hh), NEG_INF)
            s1 = jnp.where(sel_ref[hh, pl.ds(n1, 1), :] > 0.5, scores(n1, hh), NEG_INF)
            m_new = jnp.maximum(m, jnp.maximum(jnp.max(s0, axis=0, keepdims=True),
                                               jnp.max(s1, axis=0, keepdims=True)))
            alpha = jnp.exp(m - m_new)
            p0 = jnp.exp(s0 - m_new)
            p1 = jnp.exp(s1 - m_new)
            l = alpha * l + (jnp.sum(p0, axis=0, keepdims=True) + jnp.sum(p1, axis=0, keepdims=True))
            acc = alpha * acc + (pv(n0, hh, p0) + pv(n1, hh, p1))
            out.append((m_new, l, acc))
        return tuple(out)

    final = lax.fori_loop(0, (qb + 1) // 2, body, tuple(init))
    for hh in range(2):
        _, l, acc = final[hh]
        o_ref[hh * HEAD_DIM:(hh + 1) * HEAD_DIM, :] = acc / l


def _moba(qT, k_blk, v_blk, *, batch, seq):
    aw, t = qT.shape
    nb = seq // MOBA_BLOCK
    n_sel = min(MOBA_TOPK, nb - 1)
    npair = aw // (2 * HEAD_DIM)
    kern = functools.partial(_moba_kernel, nb=nb, n_sel=n_sel)
    return pl.pallas_call(
        kern,
        grid=(batch, npair, nb),
        in_specs=[
            pl.BlockSpec((2 * HEAD_DIM, MOBA_BLOCK), lambda b, hp, qb: (hp, b * nb + qb)),
            pl.BlockSpec((nb, MOBA_BLOCK, 2 * HEAD_DIM), lambda b, hp, qb: (b, 0, hp)),
            pl.BlockSpec((nb, 2 * HEAD_DIM, MOBA_BLOCK), lambda b, hp, qb: (b, hp, 0)),
        ],
        out_specs=pl.BlockSpec((2 * HEAD_DIM, MOBA_BLOCK), lambda b, hp, qb: (hp, b * nb + qb)),
        out_shape=jax.ShapeDtypeStruct((aw, t), F32),
        scratch_shapes=[pltpu.VMEM((nb, 2 * HEAD_DIM), F32),
                        pltpu.VMEM((2, nb, MOBA_BLOCK), F32),
                        pltpu.VMEM((2, 2 * HEAD_DIM, MOBA_BLOCK), BF16)],
        compiler_params=_params("parallel", "parallel", "arbitrary"),
        name="moba_attn",
    )(qT, k_blk, v_blk)


def _rms_rows(v):
    return v * lax.rsqrt(jnp.mean(v * v, axis=0, keepdims=True) + NORM_EPS)


def _outproj_kernel(yc_ref, ya_ref, x_ref, gg_ref, wo_ref, pg_ref, g1_ref, fg_ref, sc2_ref,
                    sh2_ref, wq_ref, keys_ref, x1_ref, h2_ref, s_ref):
    y = jnp.concatenate([yc_ref[...], ya_ref[...]], axis=0)
    ngroups = y.shape[0] // HEAD_DIM
    yn = jnp.concatenate(
        [_rms_rows(y[g * HEAD_DIM:(g + 1) * HEAD_DIM]) for g in range(ngroups)], axis=0)
    yn = (yn * gg_ref[...]).astype(BF16)
    o = jnp.dot(wo_ref[...], yn, preferred_element_type=F32)
    r = _rms_rows(o) * pg_ref[...]
    x1 = x_ref[...].T + g1_ref[...] * r
    x1_ref[...] = x1
    h2 = (_rms_rows(x1) * fg_ref[...]) * (1.0 + sc2_ref[...]) + sh2_ref[...]
    h2b = h2.astype(BF16)
    h2_ref[...] = h2b
    qp = jnp.dot(wq_ref[...], h2b, preferred_element_type=F32)
    dk = keys_ref.shape[2]
    for hp in range(keys_ref.shape[0]):
        s_ref[hp] = jnp.dot(keys_ref[hp], qp[hp * dk:(hp + 1) * dk].astype(BF16),
                            preferred_element_type=F32)


def _outproj(ycT, yaT, x2, gg_col, w_outT, pg_col, mod_cols, fg_col, w_pqT, keys, *, seq, tm):
    t, d = x2.shape
    cw = ycT.shape[0]
    aw = yaT.shape[0]
    tps = seq // tm
    nhp, nk, dk = keys.shape
    col = pl.BlockSpec((d, 1), lambda i: (0, 0))

    def modcol(kidx):
        return pl.BlockSpec((None, None, d, 1), lambda i: (i // tps, kidx, 0, 0))

    return pl.pallas_call(
        _outproj_kernel,
        grid=(t // tm,),
        in_specs=[
            pl.BlockSpec((cw, tm), lambda i: (0, i)),
            pl.BlockSpec((aw, tm), lambda i: (0, i)),
            pl.BlockSpec((tm, d), lambda i: (i, 0)),
            col,
            pl.BlockSpec((d, d), lambda i: (0, 0)),
            col,
            modcol(2),
            col,
            modcol(4),
            modcol(3),
            pl.BlockSpec((nhp * dk, d), lambda i: (0, 0)),
            pl.BlockSpec((nhp, nk, dk), lambda i: (0, 0, 0)),
        ],
        out_specs=[
            pl.BlockSpec((d, tm), lambda i: (0, i)),
            pl.BlockSpec((d, tm), lambda i: (0, i)),
            pl.BlockSpec((nhp, nk, tm), lambda i: (0, 0, i)),
        ],
        out_shape=[
            jax.ShapeDtypeStruct((d, t), F32),
            jax.ShapeDtypeStruct((d, t), BF16),
            jax.ShapeDtypeStruct((nhp, nk, t), F32),
        ],
        compiler_params=_params("parallel"),
        name="outproj_peerq",
    )(ycT, yaT, x2, gg_col, w_outT, pg_col, mod_cols, fg_col, mod_cols, mod_cols, w_pqT, keys)


def _top_values(s, k):
    rows = []
    for _ in range(k):
        mx = jnp.max(s, axis=0, keepdims=True)
        rows.append(mx)
        s = jnp.where(s == mx, -jnp.inf, s)
    return jnp.concatenate(rows, axis=0)


def _route_kernel(s_ref, thr_ref, e1_ref, e2_ref):
    k = PEER_TOPK
    k1 = k + 1
    s1 = s_ref[0]
    s2 = s_ref[1]
    t1 = _top_values(s1, k1)
    t2 = _top_values(s2, k1)
    m1 = t1[0:1]
    m2 = t2[0:1]
    brow8 = lax.broadcasted_iota(jnp.int32, (8, s1.shape[1]), 0)
    cands = [t1[0:1] + t2]
    for a in range(1, k1):
        cands.append(jnp.where(brow8 < k1 // (a + 1), t1[a:a + 1] + t2[0:8], -jnp.inf))
    c = jnp.concatenate(cands, axis=0)
    cnt = jnp.zeros_like(m1)
    ck = jnp.full_like(m1, -jnp.inf)
    ck1 = jnp.full_like(m1, -jnp.inf)
    for _ in range(k1):
        mx = jnp.max(c, axis=0, keepdims=True)
        eq = c == mx
        new_cnt = cnt + jnp.sum(jnp.where(eq, 1.0, 0.0), axis=0, keepdims=True)
        ck = jnp.where(cnt < k, jnp.where(new_cnt >= k, mx, ck), ck)
        ck1 = jnp.where(cnt < k1, jnp.where(new_cnt >= k1, mx, ck1), ck1)
        cnt = new_cnt
        c = jnp.where(eq, -jnp.inf, c)
    tau = 0.5 * (ck + ck1)
    et1 = jnp.exp(t1 - m1)
    et2 = jnp.exp(t2 - m2)
    z = jnp.sum(jnp.where(t2 >= tau - t1[0:1], et2, 0.0), axis=0, keepdims=True) * et1[0:1]
    for a in range(1, k1):
        wa = jnp.where(brow8 < k1 // (a + 1),
                       jnp.where(t2[0:8] >= tau - t1[a:a + 1], et2[0:8], 0.0), 0.0)
        z = z + jnp.sum(wa, axis=0, keepdims=True) * et1[a:a + 1]
    thr_ref[...] = tau - s1
    e1_ref[...] = jnp.exp(s1 - m1) / z
    e2_ref[...] = jnp.exp(s2 - m2)


def _route(s4, *, tl):
    nh, _, nk, t = s4.shape
    out = jax.ShapeDtypeStruct((nh, nk, t), F32)
    ospec = pl.BlockSpec((None, nk, tl), lambda i, h: (h, 0, i))
    return pl.pallas_call(
        _route_kernel,
        grid=(t // tl, nh),
        in_specs=[pl.BlockSpec((None, 2, nk, tl), lambda i, h: (h, 0, 0, i))],
        out_specs=[ospec, ospec, ospec],
        out_shape=[out, out, out],
        compiler_params=_params("parallel", "parallel"),
        name="peer_route",
    )(s4)


def _gelu_tanh(a):
    return 0.5 * a * (1.0 + jnp.tanh(0.7978845608028654 * (a + 0.044715 * (a * a * a))))


def _peer_kernel(h_ref, u_ref, vt_ref, s2_ref, thr_ref, e1_ref, e2_ref, y_ref, w_ref, *,
                 lane_chunk, sub):
    e = pl.program_id(1)
    et, tt = w_ref.shape
    nk = s2_ref.shape[1]
    nheads = s2_ref.shape[0]
    nsub = et // sub

    @pl.when(e == 0)
    def _():
        y_ref[...] = jnp.zeros_like(y_ref)

    tchunk = min(256, tt)
    for t0 in range(0, tt, tchunk):
        ts = slice(t0, t0 + tchunk)
        for k in range(nsub):
            ks = slice(k * sub, (k + 1) * sub)
            aT = jnp.dot(u_ref[ks, :], h_ref[:, ts], preferred_element_type=F32)
            for il in range(sub // nk):
                i = (e * et + k * sub) // nk + il
                r0 = k * sub + il * nk
                thr_rows = [thr_ref[h, pl.ds(i, 1), ts] for h in range(nheads)]
                e1_rows = [e1_ref[h, pl.ds(i, 1), ts] for h in range(nheads)]
                for c0 in range(0, tchunk, lane_chunk):
                    cs = slice(c0, c0 + lane_chunk)
                    gs = slice(t0 + c0, t0 + c0 + lane_chunk)
                    g = jnp.zeros((nk, lane_chunk), F32)
                    for h in range(nheads):
                        g = g + (jnp.where(s2_ref[h, :, gs] >= thr_rows[h][:, cs],
                                           e2_ref[h, :, gs], 0.0) * e1_rows[h][:, cs])
                    a = aT[il * nk:(il + 1) * nk, cs]
                    w_ref[r0:r0 + nk, gs] = (g * _gelu_tanh(a)).astype(BF16)
            y_ref[:, ts] += jnp.dot(vt_ref[:, ks], w_ref[ks, ts], preferred_element_type=F32)


def _peer(h2T, u_b, vt_b, s4, thr, e1, e2, *, tt, et):
    d, t = h2T.shape
    ne = u_b.shape[0]
    nh, _, nk, _ = s4.shape
    stat = pl.BlockSpec((nh, nk, tt), lambda i, e: (0, 0, i))
    kern = functools.partial(_peer_kernel, lane_chunk=min(128, tt), sub=min(256, et))
    return pl.pallas_call(
        kern,
        grid=(t // tt, ne // et),
        in_specs=[
            pl.BlockSpec((d, tt), lambda i, e: (0, i)),
            pl.BlockSpec((et, d), lambda i, e: (e, 0)),
            pl.BlockSpec((d, et), lambda i, e: (0, e)),
            pl.BlockSpec((nh, None, nk, tt), lambda i, e: (0, 1, 0, i)),
            stat, stat, stat,
        ],
        out_specs=pl.BlockSpec((d, tt), lambda i, e: (0, i)),
        out_shape=jax.ShapeDtypeStruct((d, t), F32),
        scratch_shapes=[pltpu.VMEM((et, tt), BF16)],
        compiler_params=_params("parallel", "arbitrary"),
        name="peer_dense",
    )(h2T, u_b, vt_b, s4, thr, e1, e2)


def _final_kernel(x1_ref, y_ref, pg_ref, g2_ref, o_ref):
    out = x1_ref[...] + g2_ref[...] * (_rms_rows(y_ref[...]) * pg_ref[...])
    o_ref[...] = out.T


def _final(x1T, yT, pg_col, mod_cols, *, seq, tm):
    d, t = x1T.shape
    tps = seq // tm
    return pl.pallas_call(
        _final_kernel,
        grid=(t // tm,),
        in_specs=[
            pl.BlockSpec((d, tm), lambda i: (0, i)),
            pl.BlockSpec((d, tm), lambda i: (0, i)),
            pl.BlockSpec((d, 1), lambda i: (0, 0)),
            pl.BlockSpec((None, None, d, 1), lambda i: (i // tps, 5, 0, 0)),
        ],
        out_specs=pl.BlockSpec((tm, d), lambda i: (i, 0)),
        out_shape=jax.ShapeDtypeStruct((t, d), F32),
        compiler_params=_params("parallel"),
        name="final_residual",
    )(x1T, yT, pg_col, mod_cols)


def _layer(x2, c_pad, ada_w, ada_b, mix_pre_gain, mix_post_gain, w_in, conv_w, group_gain, w_out,
           ffn_pre_gain, ffn_post_gain, peer_wq, peer_keys, peer_u, peer_v, *, batch, seq):
    t, d = x2.shape
    cw = conv_w.shape[-1]
    aw = d - cw
    tm = min(512, seq)

    mod = _modulation(c_pad, ada_w, ada_b)[:batch]
    mod_rows = mod.reshape(batch, 6, 1, d)
    mod_cols = mod.reshape(batch, 6, d, 1)

    w_rest = jnp.concatenate([w_in[:, :3 * cw + aw], w_in[:, 3 * cw + 2 * aw:]], axis=1)
    w_t = w_rest.T.astype(BF16)
    w_k = w_in[:, 3 * cw + aw:3 * cw + 2 * aw].astype(BF16)
    conv_cols = conv_w.reshape(conv_w.shape[0], cw, 1)
    w_outT = w_out.T.astype(BF16)
    w_pqT = peer_wq.T.astype(BF16)
    nh, _, nk, dk = peer_keys.shape
    keys = peer_keys.reshape(nh * 2, nk, dk).astype(BF16)
    u_b = peer_u.astype(BF16)
    vt_b = peer_v.T.astype(BF16)

    ycT, qT, k_nat, v_blk = _inproj(x2, mod_rows, mix_pre_gain.reshape(1, d), w_t, w_k, conv_cols,
                                    seq=seq, tm=tm)
    k_blk = k_nat.reshape(t // MOBA_BLOCK, MOBA_BLOCK, aw)
    yaT = _moba(qT, k_blk, v_blk, batch=batch, seq=seq)
    x1T, h2T, sT = _outproj(ycT, yaT, x2, group_gain.reshape(d, 1), w_outT,
                            mix_post_gain.reshape(d, 1), mod_cols, ffn_pre_gain.reshape(d, 1),
                            w_pqT, keys, seq=seq, tm=tm)
    s4 = sT.reshape(nh, 2, nk, t)
    thr, e1, e2 = _route(s4, tl=min(512, t))
    yT = _peer(h2T, u_b, vt_b, s4, thr, e1, e2, tt=min(512, t), et=min(1024, u_b.shape[0]))
    return _final(x1T, yT, ffn_post_gain.reshape(d, 1), mod_cols, seq=seq, tm=tm)


def kernel(x, c, ada_w, ada_b, mix_pre_gain, mix_post_gain, w_in, conv_w, group_gain, w_out,
           ffn_pre_gain, ffn_post_gain, peer_wq, peer_keys, peer_u, peer_v):
    batch, seq, d = x.shape
    x2 = x.reshape(batch * seq, d)
    c_pad = jnp.zeros((8, d), c.dtype).at[:batch].set(c)
    for l in range(ada_w.shape[0]):
        x2 = _layer(x2, c_pad, ada_w[l], ada_b[l], mix_pre_gain[l], mix_post_gain[l], w_in[l],
                    conv_w[l], group_gain[l], w_out[l], ffn_pre_gain[l], ffn_post_gain[l],
                    peer_wq[l], peer_keys[l], peer_u[l], peer_v[l], batch=batch, seq=seq)
    return x2.reshape(batch, seq, d)
```

```python
import functools
import math

import jax
import jax.numpy as jnp
from jax import lax
from jax.experimental import pallas as pl
from jax.experimental.pallas import tpu as pltpu

F32 = jnp.float32
BF16 = jnp.bfloat16

HEAD_DIM = 64
MOBA_BLOCK = 256
MOBA_TOPK = 3
PEER_TOPK = 16
NORM_EPS = 1e-6
NEG_INF = -1e30
LANES = 128
VMEM_LIMIT = 56 * 1024 * 1024


def _params(*sem):
    return pltpu.CompilerParams(dimension_semantics=sem, vmem_limit_bytes=VMEM_LIMIT)


def _mod_kernel(c_ref, w_ref, b_ref, o_ref):
    c = c_ref[...]
    s = c * jax.nn.sigmoid(c)
    o_ref[...] = jnp.dot(s, w_ref[...], preferred_element_type=F32,
                         precision=lax.Precision.HIGHEST) + b_ref[...]


def _modulation(c_pad, w, b):
    rows, d = c_pad.shape
    n = w.shape[1]
    tn = 768 if n % 768 == 0 else n
    return pl.pallas_call(
        _mod_kernel,
        grid=(n // tn,),
        in_specs=[pl.BlockSpec((rows, d), lambda j: (0, 0)),
                  pl.BlockSpec((d, tn), lambda j: (0, j)),
                  pl.BlockSpec((1, tn), lambda j: (0, j))],
        out_specs=pl.BlockSpec((rows, tn), lambda j: (0, j)),
        out_shape=jax.ShapeDtypeStruct((rows, n), F32),
        compiler_params=_params("parallel"),
        name="adaln_mod",
    )(c_pad, w, b.reshape(1, n))


def _inproj_kernel(x_ref, sh_ref, sc_ref, gain_ref, wt_ref, wk_ref, cw_ref,
                   yc_ref, q_ref, k_ref, v_ref, carry_ref, *, tps, cw, aw):
    i = pl.program_id(0)
    x = x_ref[...]
    tm = x.shape[0]
    ms = jnp.mean(x * x, axis=-1, keepdims=True)
    h = x * lax.rsqrt(ms + NORM_EPS) * gain_ref[...]
    h = h * (1.0 + sc_ref[...]) + sh_ref[...]
    hb = h.astype(BF16)
    k_ref[...] = jnp.dot(hb, wk_ref[...], preferred_element_type=F32).astype(BF16)
    pT = lax.dot_general(wt_ref[...], hb, (((1,), (1,)), ((), ())),
                         preferred_element_type=F32)
    gate_b = pT[0:cw]
    gate_c = pT[cw:2 * cw]
    hx = pT[2 * cw:3 * cw]
    q_ref[...] = pT[3 * cw:3 * cw + aw].astype(BF16)
    vT = pT[3 * cw + aw:3 * cw + 2 * aw]
    for j in range(tm // MOBA_BLOCK):
        v_ref[j] = vT[:, j * MOBA_BLOCK:(j + 1) * MOBA_BLOCK].astype(BF16)
    u = gate_c * hx
    prev = jnp.where(i % tps == 0, 0.0, carry_ref[...])
    carry_ref[...] = u[:, tm - LANES:]
    ue = jnp.concatenate([prev, u], axis=1)
    u1 = pltpu.roll(ue, 1, axis=1)[:, LANES:]
    u2 = pltpu.roll(ue, 2, axis=1)[:, LANES:]
    conv = cw_ref[0] * u2 + cw_ref[1] * u1 + cw_ref[2] * u
    yc_ref[...] = gate_b * conv


def _inproj(x2, mod_rows, gain_row, w_t, w_k, conv_cols, *, seq, tm):
    t, d = x2.shape
    cw = conv_cols.shape[1]
    aw = w_k.shape[1]
    tps = seq // tm
    nrest = w_t.shape[0]
    kern = functools.partial(_inproj_kernel, tps=tps, cw=cw, aw=aw)
    return pl.pallas_call(
        kern,
        grid=(t // tm,),
        in_specs=[
            pl.BlockSpec((tm, d), lambda i: (i, 0)),
            pl.BlockSpec((None, None, 1, d), lambda i: (i // tps, 0, 0, 0)),
            pl.BlockSpec((None, None, 1, d), lambda i: (i // tps, 1, 0, 0)),
            pl.BlockSpec((1, d), lambda i: (0, 0)),
            pl.BlockSpec((nrest, d), lambda i: (0, 0)),
            pl.BlockSpec((d, aw), lambda i: (0, 0)),
            pl.BlockSpec((3, cw, 1), lambda i: (0, 0, 0)),
        ],
        out_specs=[
            pl.BlockSpec((cw, tm), lambda i: (0, i)),
            pl.BlockSpec((aw, tm), lambda i: (0, i)),
            pl.BlockSpec((tm, aw), lambda i: (i, 0)),
            pl.BlockSpec((tm // MOBA_BLOCK, aw, MOBA_BLOCK), lambda i: (i, 0, 0)),
        ],
        out_shape=[
            jax.ShapeDtypeStruct((cw, t), F32),
            jax.ShapeDtypeStruct((aw, t), BF16),
            jax.ShapeDtypeStruct((t, aw), BF16),
            jax.ShapeDtypeStruct((t // MOBA_BLOCK, aw, MOBA_BLOCK), BF16),
        ],
        scratch_shapes=[pltpu.VMEM((cw, LANES), F32)],
        compiler_params=_params("arbitrary"),
        name="inproj_conv",
    )(x2, mod_rows, mod_rows, gain_row, w_t, w_k, conv_cols)


def _moba_kernel(q_ref, k_ref, v_ref, o_ref, km_ref, sel_ref, qm_ref, *, nb, n_sel):
    qb = pl.program_id(2)
    blk = MOBA_BLOCK
    scale = HEAD_DIM ** -0.5
    fold_scale = float(math.log2(scale)).is_integer()
    post_scale = 1.0 if fold_scale else scale

    @pl.when(qb == 0)
    def _():
        for n in range(nb):
            km_ref[pl.ds(n, 1), :] = jnp.mean(k_ref[n].astype(F32), axis=0, keepdims=True)

    q2 = q_ref[...].astype(F32)
    if fold_scale:
        q2 = q2 * scale
    row = lax.broadcasted_iota(jnp.int32, q2.shape, 0)
    km = km_ref[...]
    km_hi = km.astype(BF16)
    km_lo = (km - km_hi.astype(F32)).astype(BF16)
    nidx = lax.broadcasted_iota(jnp.int32, (nb, blk), 0)
    past = nidx < qb
    causal = (lax.broadcasted_iota(jnp.int32, (blk, blk), 0)
              <= lax.broadcasted_iota(jnp.int32, (blk, blk), 1))

    def scores(n, hh):
        s = jnp.dot(k_ref[n], qm_ref[hh], preferred_element_type=F32)
        return s if fold_scale else s * post_scale

    def pv(n, hh, p):
        return jnp.dot(v_ref[n, hh * HEAD_DIM:(hh + 1) * HEAD_DIM, :], p.astype(BF16),
                       preferred_element_type=F32)

    init = []
    for hh in range(2):
        qm = jnp.where((row >= hh * HEAD_DIM) & (row < (hh + 1) * HEAD_DIM), q2, 0.0).astype(BF16)
        qm_ref[hh] = qm
        gate = (jnp.dot(km_hi, qm, preferred_element_type=F32)
                + jnp.dot(km_lo, qm, preferred_element_type=F32))
        g = jnp.where(past, gate, NEG_INF)
        sel = jnp.zeros((nb, blk), F32)
        for _ in range(n_sel):
            mx = jnp.max(g, axis=0, keepdims=True)
            idx = jnp.min(jnp.where(g == mx, nidx, nb), axis=0, keepdims=True)
            pick = nidx == idx
            sel = jnp.where(pick, 1.0, sel)
            g = jnp.where(pick, -3e38, g)
        sel_ref[hh] = jnp.where(past, sel, 0.0)
        s = jnp.where(causal, scores(qb, hh), NEG_INF)
        m0 = jnp.max(s, axis=0, keepdims=True)
        p = jnp.exp(s - m0)
        init.append((m0, jnp.sum(p, axis=0, keepdims=True), pv(qb, hh, p)))

    grp = 4 if nb % 4 == 0 else (2 if nb % 2 == 0 else 1)

    def body(j, carry):
        out = []
        for hh in range(2):
            m, l, acc = carry[hh]
            ss = []
            m_new = m
            for r in range(grp):
                n = grp * j + r
                s = jnp.where(sel_ref[hh, pl.ds(n, 1), :] > 0.5, scores(n, hh), NEG_INF)
                ss.append((n, s))
                m_new = jnp.maximum(m_new, jnp.max(s, axis=0, keepdims=True))
            alpha = jnp.exp(m - m_new)
            lsum = None
            asum = None
            for n, s in ss:
                p = jnp.exp(s - m_new)
                ps = jnp.sum(p, axis=0, keepdims=True)
                pa = pv(n, hh, p)
                lsum = ps if lsum is None else lsum + ps
                asum = pa if asum is None else asum + pa
            out.append((m_new, alpha * l + lsum, alpha * acc + asum))
        return tuple(out)

    final = lax.fori_loop(0, (qb + grp - 1) // grp, body, tuple(init))
    for hh in range(2):
        _, l, acc = final[hh]
        o_ref[hh * HEAD_DIM:(hh + 1) * HEAD_DIM, :] = acc / l


def _moba(qT, k_blk, v_blk, *, batch, seq):
    aw, t = qT.shape
    nb = seq // MOBA_BLOCK
    n_sel = min(MOBA_TOPK, nb - 1)
    npair = aw // (2 * HEAD_DIM)
    kern = functools.partial(_moba_kernel, nb=nb, n_sel=n_sel)
    return pl.pallas_call(
        kern,
        grid=(batch, npair, nb),
        in_specs=[
            pl.BlockSpec((2 * HEAD_DIM, MOBA_BLOCK), lambda b, hp, qb: (hp, b * nb + qb)),
            pl.BlockSpec((nb, MOBA_BLOCK, 2 * HEAD_DIM), lambda b, hp, qb: (b, 0, hp)),
            pl.BlockSpec((nb, 2 * HEAD_DIM, MOBA_BLOCK), lambda b, hp, qb: (b, hp, 0)),
        ],
        out_specs=pl.BlockSpec((2 * HEAD_DIM, MOBA_BLOCK), lambda b, hp, qb: (hp, b * nb + qb)),
        out_shape=jax.ShapeDtypeStruct((aw, t), F32),
        scratch_shapes=[pltpu.VMEM((nb, 2 * HEAD_DIM), F32),
                        pltpu.VMEM((2, nb, MOBA_BLOCK), F32),
                        pltpu.VMEM((2, 2 * HEAD_DIM, MOBA_BLOCK), BF16)],
        compiler_params=_params("parallel", "parallel", "arbitrary"),
        name="moba_attn",
    )(qT, k_blk, v_blk)


def _rms_rows(v):
    return v * lax.rsqrt(jnp.mean(v * v, axis=0, keepdims=True) + NORM_EPS)


def _outproj_kernel(yc_ref, ya_ref, x_ref, gg_ref, wo_ref, pg_ref, g1_ref, fg_ref, sc2_ref,
                    sh2_ref, wq_ref, keys_ref, x1_ref, h2_ref, s_ref):
    y = jnp.concatenate([yc_ref[...], ya_ref[...]], axis=0)
    ngroups = y.shape[0] // HEAD_DIM
    yn = jnp.concatenate(
        [_rms_rows(y[g * HEAD_DIM:(g + 1) * HEAD_DIM]) for g in range(ngroups)], axis=0)
    yn = (yn * gg_ref[...]).astype(BF16)
    o = jnp.dot(wo_ref[...], yn, preferred_element_type=F32)
    r = _rms_rows(o) * pg_ref[...]
    x1 = x_ref[...].T + g1_ref[...] * r
    x1_ref[...] = x1
    h2 = (_rms_rows(x1) * fg_ref[...]) * (1.0 + sc2_ref[...]) + sh2_ref[...]
    h2b = h2.astype(BF16)
    h2_ref[...] = h2b
    qp = jnp.dot(wq_ref[...], h2b, preferred_element_type=F32)
    dk = keys_ref.shape[2]
    for hp in range(keys_ref.shape[0]):
        s_ref[hp] = jnp.dot(keys_ref[hp], qp[hp * dk:(hp + 1) * dk].astype(BF16),
                            preferred_element_type=F32)


def _outproj(ycT, yaT, x2, gg_col, w_outT, pg_col, mod_cols, fg_col, w_pqT, keys, *, seq, tm):
    t, d = x2.shape
    cw = ycT.shape[0]
    aw = yaT.shape[0]
    tps = seq // tm
    nhp, nk, dk = keys.shape
    col = pl.BlockSpec((d, 1), lambda i: (0, 0))

    def modcol(kidx):
        return pl.BlockSpec((None, None, d, 1), lambda i: (i // tps, kidx, 0, 0))

    return pl.pallas_call(
        _outproj_kernel,
        grid=(t // tm,),
        in_specs=[
            pl.BlockSpec((cw, tm), lambda i: (0, i)),
            pl.BlockSpec((aw, tm), lambda i: (0, i)),
            pl.BlockSpec((tm, d), lambda i: (i, 0)),
            col,
            pl.BlockSpec((d, d), lambda i: (0, 0)),
            col,
            modcol(2),
            col,
            modcol(4),
            modcol(3),
            pl.BlockSpec((nhp * dk, d), lambda i: (0, 0)),
            pl.BlockSpec((nhp, nk, dk), lambda i: (0, 0, 0)),
        ],
        out_specs=[
            pl.BlockSpec((d, tm), lambda i: (0, i)),
            pl.BlockSpec((d, tm), lambda i: (0, i)),
            pl.BlockSpec((nhp, nk, tm), lambda i: (0, 0, i)),
        ],
        out_shape=[
            jax.ShapeDtypeStruct((d, t), F32),
            jax.ShapeDtypeStruct((d, t), BF16),
            jax.ShapeDtypeStruct((nhp, nk, t), F32),
        ],
        compiler_params=_params("parallel"),
        name="outproj_peerq",
    )(ycT, yaT, x2, gg_col, w_outT, pg_col, mod_cols, fg_col, mod_cols, mod_cols, w_pqT, keys)


SUBLANES = 8


def _oddeven_merge(lo, hi, r):
    step = r * 2
    if step < hi - lo:
        yield from _oddeven_merge(lo, hi, step)
        yield from _oddeven_merge(lo + r, hi, step)
        yield from [(i, i + r) for i in range(lo + r, hi - r, step)]
    else:
        yield (lo, lo + r)


def _oddeven_sort_pairs(lo, hi):
    if hi - lo >= 1:
        mid = lo + (hi - lo) // 2
        yield from _oddeven_sort_pairs(lo, mid)
        yield from _oddeven_sort_pairs(mid + 1, hi)
        yield from _oddeven_merge(lo, hi, 1)


def _top_values(s, k):
    n = s.shape[0] // SUBLANES
    v = [s[SUBLANES * i:SUBLANES * (i + 1)] for i in range(n)]
    for i, j in _oddeven_sort_pairs(0, n - 1):
        v[i], v[j] = jnp.maximum(v[i], v[j]), jnp.minimum(v[i], v[j])
    nslab = -(-k // SUBLANES)
    row = lax.broadcasted_iota(jnp.int32, (nslab * SUBLANES, s.shape[1]), 0)
    stacked = jnp.full((nslab * SUBLANES, s.shape[1]), -jnp.inf, F32)
    empty = jnp.full_like(v[0], -jnp.inf)
    tops = []
    for a in range(k):
        mx = jnp.max(v[0], axis=0, keepdims=True)
        tops.append(mx)
        stacked = jnp.where(row == a, mx, stacked)
        won = v[0] == mx
        for lvl in range(min(n, k - 1 - a)):
            v[lvl] = jnp.where(won, v[lvl + 1] if lvl + 1 < n else empty, v[lvl])
    return tops, stacked


def _route_kernel(s_ref, thr_ref, e1_ref, e2_ref):
    k = PEER_TOPK
    assert k == 16, "the candidate tiling below enumerates (a+1)(b+1) <= 17"
    k1 = k + 1
    s1 = s_ref[0]
    s2 = s_ref[1]
    r1, t1 = _top_values(s1, k1)
    r2, t2 = _top_values(s2, k1)
    m1, m2 = r1[0], r2[0]
    t1a, t1b, t2a, t2b = t1[0:8], t1[8:16], t2[0:8], t2[8:16]
    e1s = jnp.exp(t1 - m1)
    e2s = jnp.exp(t2 - m2)
    e1a, e1b, e2a, e2b = e1s[0:8], e1s[8:16], e2s[0:8], e2s[8:16]
    row = lax.broadcasted_iota(jnp.int32, t1a.shape, 0)
    ninf = -jnp.inf

    def pick3(x0, x1, x2, fill):
        return jnp.where(row == 0, x0, jnp.where(row == 1, x1, jnp.where(row == 2, x2, fill)))

    tiles = [
        (t1a, r2[0], e1a, e2s[0:1], None),
        (t1b, r2[0], e1b, e2s[0:1], None),
        (t1a, r2[1], e1a, e2s[1:2], None),
        (r1[0], t2b, e1s[0:1], e2b, None),
        (r1[0], t2a, e1s[0:1], e2a, row >= 2),
        (r1[1], t2a, e1s[1:2], e2a, row >= 2),
        (t1a, r2[2], e1a, e2s[2:3], (row >= 2) & (row <= 4)),
        (t1a, r2[3], e1a, e2s[3:4], (row >= 2) & (row <= 3)),
        (pick3(r1[16], r1[0], r1[2], ninf), pick3(r2[0], r2[16], r2[4], ninf),
         pick3(e1s[16:17], e1s[0:1], e1s[2:3], 0.0), pick3(e2s[0:1], e2s[16:17], e2s[4:5], 0.0),
         row <= 2),
    ]
    cands = []
    for ta, tb, _, _, used in tiles:
        c = ta + tb
        cands.append(c if used is None else jnp.where(used, c, ninf))
    cnt = jnp.zeros_like(m1)
    ck = jnp.full_like(m1, ninf)
    ck1 = jnp.full_like(m1, ninf)
    for _ in range(k1):
        mx = cands[0]
        for c in cands[1:]:
            mx = jnp.maximum(mx, c)
        mx = jnp.max(mx, axis=0, keepdims=True)
        eqs = [c == mx for c in cands]
        n_eq = jnp.where(eqs[0], 1.0, 0.0)
        for eq in eqs[1:]:
            n_eq = n_eq + jnp.where(eq, 1.0, 0.0)
        new_cnt = cnt + jnp.sum(n_eq, axis=0, keepdims=True)
        ck = jnp.where(cnt < k, jnp.where(new_cnt >= k, mx, ck), ck)
        ck1 = jnp.where(cnt < k1, jnp.where(new_cnt >= k1, mx, ck1), ck1)
        cnt = new_cnt
        cands = [jnp.where(eq, ninf, c) for eq, c in zip(eqs, cands)]
    tau = 0.5 * (ck + ck1)
    zacc = None
    for ta, tb, ea, eb, used in tiles:
        wgt = jnp.where(tb >= tau - ta, ea * eb, 0.0)
        if used is not None:
            wgt = jnp.where(used, wgt, 0.0)
        zacc = wgt if zacc is None else zacc + wgt
    inv_z = 1.0 / jnp.sum(zacc, axis=0, keepdims=True)
    thr_ref[...] = tau - s1
    e1_ref[...] = jnp.exp(s1 - m1) * inv_z
    e2_ref[...] = jnp.exp(s2 - m2)


def _route(s4, *, tl):
    nh, _, nk, t = s4.shape
    out = jax.ShapeDtypeStruct((nh, nk, t), F32)
    ospec = pl.BlockSpec((None, nk, tl), lambda i, h: (h, 0, i))
    return pl.pallas_call(
        _route_kernel,
        grid=(t // tl, nh),
        in_specs=[pl.BlockSpec((None, 2, nk, tl), lambda i, h: (h, 0, 0, i))],
        out_specs=[ospec, ospec, ospec],
        out_shape=[out, out, out],
        compiler_params=_params("parallel", "parallel"),
        name="peer_route",
    )(s4)


GELU_C1 = 0.7978845608028654
GELU_C2 = 0.044715


def _gated_gelu(g, a):
    t = jnp.tanh(a * (GELU_C1 + (GELU_C1 * GELU_C2) * (a * a)))
    hg = (0.5 * a) * g
    return hg + hg * t


def _peer_kernel(h_ref, u_ref, vt_ref, s2_ref, thr_ref, e1_ref, e2_ref, y_ref, w_ref, *,
                 lane_chunk, sub):
    e = pl.program_id(1)
    et, tt = w_ref.shape
    nk = s2_ref.shape[1]
    nheads = s2_ref.shape[0]
    nsub = et // sub

    @pl.when(e == 0)
    def _():
        y_ref[...] = jnp.zeros_like(y_ref)

    tchunk = min(256, tt)
    for t0 in range(0, tt, tchunk):
        ts = slice(t0, t0 + tchunk)
        for k in range(nsub):
            ks = slice(k * sub, (k + 1) * sub)
            aT = jnp.dot(u_ref[ks, :], h_ref[:, ts], preferred_element_type=F32)
            for il in range(sub // nk):
                i = (e * et + k * sub) // nk + il
                r0 = k * sub + il * nk
                thr_rows = [thr_ref[h, pl.ds(i, 1), ts] for h in range(nheads)]
                e1_rows = [e1_ref[h, pl.ds(i, 1), ts] for h in range(nheads)]
                for c0 in range(0, tchunk, lane_chunk):
                    cs = slice(c0, c0 + lane_chunk)
                    gs = slice(t0 + c0, t0 + c0 + lane_chunk)
                    g = None
                    for h in range(nheads):
                        gh = (jnp.where(s2_ref[h, :, gs] >= thr_rows[h][:, cs],
                                        e2_ref[h, :, gs], 0.0) * e1_rows[h][:, cs])
                        g = gh if g is None else g + gh
                    a = aT[il * nk:(il + 1) * nk, cs]
                    w_ref[r0:r0 + nk, gs] = _gated_gelu(g, a).astype(BF16)
            y_ref[:, ts] += jnp.dot(vt_ref[:, ks], w_ref[ks, ts], preferred_element_type=F32)


def _peer(h2T, u_b, vt_b, s4, thr, e1, e2, *, tt, et):
    d, t = h2T.shape
    ne = u_b.shape[0]
    nh, _, nk, _ = s4.shape
    stat = pl.BlockSpec((nh, nk, tt), lambda i, e: (0, 0, i))
    kern = functools.partial(_peer_kernel, lane_chunk=min(128, tt), sub=min(256, et))
    return pl.pallas_call(
        kern,
        grid=(t // tt, ne // et),
        in_specs=[
            pl.BlockSpec((d, tt), lambda i, e: (0, i)),
            pl.BlockSpec((et, d), lambda i, e: (e, 0)),
            pl.BlockSpec((d, et), lambda i, e: (0, e)),
            pl.BlockSpec((nh, None, nk, tt), lambda i, e: (0, 1, 0, i)),
            stat, stat, stat,
        ],
        out_specs=pl.BlockSpec((d, tt), lambda i, e: (0, i)),
        out_shape=jax.ShapeDtypeStruct((d, t), F32),
        scratch_shapes=[pltpu.VMEM((et, tt), BF16)],
        compiler_params=_params("parallel", "arbitrary"),
        name="peer_dense",
    )(h2T, u_b, vt_b, s4, thr, e1, e2)


def _final_kernel(x1_ref, y_ref, pg_ref, g2_ref, o_ref):
    out = x1_ref[...] + g2_ref[...] * (_rms_rows(y_ref[...]) * pg_ref[...])
    o_ref[...] = out.T


def _final(x1T, yT, pg_col, mod_cols, *, seq, tm):
    d, t = x1T.shape
    tps = seq // tm
    return pl.pallas_call(
        _final_kernel,
        grid=(t // tm,),
        in_specs=[
            pl.BlockSpec((d, tm), lambda i: (0, i)),
            pl.BlockSpec((d, tm), lambda i: (0, i)),
            pl.BlockSpec((d, 1), lambda i: (0, 0)),
            pl.BlockSpec((None, None, d, 1), lambda i: (i // tps, 5, 0, 0)),
        ],
        out_specs=pl.BlockSpec((tm, d), lambda i: (i, 0)),
        out_shape=jax.ShapeDtypeStruct((t, d), F32),
        compiler_params=_params("parallel"),
        name="final_residual",
    )(x1T, yT, pg_col, mod_cols)


def _layer(x2, c_pad, ada_w, ada_b, mix_pre_gain, mix_post_gain, w_in, conv_w, group_gain, w_out,
           ffn_pre_gain, ffn_post_gain, peer_wq, peer_keys, peer_u, peer_v, *, batch, seq):
    t, d = x2.shape
    cw = conv_w.shape[-1]
    aw = d - cw
    tm = min(512, seq)

    mod = _modulation(c_pad, ada_w, ada_b)[:batch]
    mod_rows = mod.reshape(batch, 6, 1, d)
    mod_cols = mod.reshape(batch, 6, d, 1)

    w_rest = jnp.concatenate([w_in[:, :3 * cw + aw], w_in[:, 3 * cw + 2 * aw:]], axis=1)
    w_t = w_rest.T.astype(BF16)
    w_k = w_in[:, 3 * cw + aw:3 * cw + 2 * aw].astype(BF16)
    conv_cols = conv_w.reshape(conv_w.shape[0], cw, 1)
    w_outT = w_out.T.astype(BF16)
    w_pqT = peer_wq.T.astype(BF16)
    nh, _, nk, dk = peer_keys.shape
    keys = peer_keys.reshape(nh * 2, nk, dk).astype(BF16)
    u_b = peer_u.astype(BF16)
    vt_b = peer_v.T.astype(BF16)

    ycT, qT, k_nat, v_blk = _inproj(x2, mod_rows, mix_pre_gain.reshape(1, d), w_t, w_k, conv_cols,
                                    seq=seq, tm=tm)
    k_blk = k_nat.reshape(t // MOBA_BLOCK, MOBA_BLOCK, aw)
    yaT = _moba(qT, k_blk, v_blk, batch=batch, seq=seq)
    x1T, h2T, sT = _outproj(ycT, yaT, x2, group_gain.reshape(d, 1), w_outT,
                            mix_post_gain.reshape(d, 1), mod_cols, ffn_pre_gain.reshape(d, 1),
                            w_pqT, keys, seq=seq, tm=tm)
    s4 = sT.reshape(nh, 2, nk, t)
    thr, e1, e2 = _route(s4, tl=min(512, t))
    yT = _peer(h2T, u_b, vt_b, s4, thr, e1, e2, tt=min(512, t), et=min(1024, u_b.shape[0]))
    return _final(x1T, yT, ffn_post_gain.reshape(d, 1), mod_cols, seq=seq, tm=tm)


def kernel(x, c, ada_w, ada_b, mix_pre_gain, mix_post_gain, w_in, conv_w, group_gain, w_out,
           ffn_pre_gain, ffn_post_gain, peer_wq, peer_keys, peer_u, peer_v):
    batch, seq, d = x.shape
    x2 = x.reshape(batch * seq, d)
    c_pad = jnp.zeros((8, d), c.dtype).at[:batch].set(c)
    for l in range(ada_w.shape[0]):
        x2 = _layer(x2, c_pad, ada_w[l], ada_b[l], mix_pre_gain[l], mix_post_gain[l], w_in[l],
                    conv_w[l], group_gain[l], w_out[l], ffn_pre_gain[l], ffn_post_gain[l],
                    peer_wq[l], peer_keys[l], peer_u[l], peer_v[l], batch=batch, seq=seq)
    return x2.reshape(batch, seq, d)
```
